```python
import math
import jax, jax.numpy as jnp
from jax import lax
import numpy as np

D_MODEL = 4096
BATCH = 2
SEQ = 8192
DEPTH = 2

N_A_LAYERS = DEPTH // 2
N_B_LAYERS = DEPTH - N_A_LAYERS
RET_HEADS = 16
RET_HEAD_DIM = D_MODEL // RET_HEADS
RET_CHUNK = 128
RET_DECAY_BASE = 5.0
ROPE_BASE = 10000.0
DIFF_HEADS = 16
DIFF_HEAD_DIM = D_MODEL // DIFF_HEADS // 2
DIFF_Q_BLOCK = 128
FFN_HIDDEN = ((8 * D_MODEL + 767) // 768) * 256
NORM_EPS = 1e-6
HEAD_NORM_EPS = 1e-5

kernel_name = 'yoco_retention_diffattn_sandwich'


def rms_norm(x, g, eps=NORM_EPS):
    xf = x.astype(jnp.float32)
    y = xf * lax.rsqrt(jnp.mean(xf * xf, axis=-1, keepdims=True) + eps)
    return (y * g.astype(jnp.float32)).astype(x.dtype)


def rotary(t):
    s, dh = t.shape[1], t.shape[-1]
    inv_freq = 1.0 / (ROPE_BASE ** jnp.linspace(0.0, 1.0, dh // 2, dtype=jnp.float32))
    ang = jnp.arange(s, dtype=jnp.float32)[:, None] * inv_freq[None, :]
    cos, sin = jnp.cos(ang)[:, None, :], jnp.sin(ang)[:, None, :]
    t1, t2 = t[..., 0::2], t[..., 1::2]
    return jnp.stack([t1 * cos - t2 * sin, t1 * sin + t2 * cos], axis=-1).reshape(t.shape)


def retention(xn, w_in, w_out):
    b, s, d = xn.shape
    H, dh, L = RET_HEADS, RET_HEAD_DIM, RET_CHUNK
    nc = s // L
    f32 = jnp.float32
    q, k, v, g = jnp.split(xn @ w_in, 4, axis=-1)
    q = rotary(q.astype(f32).reshape(b, s, H, dh))
    k = rotary(k.astype(f32).reshape(b, s, H, dh)) * (dh ** -0.5)
    v = v.astype(f32).reshape(b, s, H, dh)

    log_g = jnp.log1p(-jnp.exp2(-RET_DECAY_BASE - jnp.arange(H, dtype=f32)))
    n = jnp.arange(L, dtype=f32)
    rel = n[:, None] - n[None, :]
    intra = jnp.where(rel >= 0, jnp.exp(log_g[:, None, None] * rel), 0.0)
    q_decay = jnp.exp(log_g[:, None] * (n + 1.0))
    k_decay = jnp.exp(log_g[:, None] * (L - 1.0 - n))
    chunk_decay = jnp.exp(log_g * L)

    def to_chunks(t):
        return t.reshape(b, nc, L, H, dh).transpose(1, 0, 3, 2, 4)

    def step(state, inp):
        qc, kc, vc = inp
        scores = jnp.einsum('bhnk,bhmk->bhnm', qc, kc) * intra
        out = (jnp.einsum('bhnm,bhmv->bhnv', scores, vc)
               + jnp.einsum('bhnk,bhkv->bhnv', qc * q_decay[:, :, None], state))
        state = (state * chunk_decay[:, None, None]
                 + jnp.einsum('bhmk,bhmv->bhkv', kc * k_decay[:, :, None], vc))
        return state, out

    state0 = jnp.zeros((b, H, dh, dh), f32)
    _, o = lax.scan(step, state0, (to_chunks(q), to_chunks(k), to_chunks(v)))
    o = o.transpose(1, 0, 3, 2, 4).reshape(b, s, H, dh)
    mu = jnp.mean(o, axis=-1, keepdims=True)
    var = jnp.mean(jnp.square(o - mu), axis=-1, keepdims=True)
    o = (o - mu) * lax.rsqrt(var + HEAD_NORM_EPS)
    o = o.reshape(b, s, d) * jax.nn.silu(g.astype(f32))
    return o.astype(xn.dtype) @ w_out


def shared_kv(h, kv_norm_g, kv_w):
    b, s, d = h.shape
    kv = (rms_norm(h, kv_norm_g) @ kv_w).astype(jnp.float32)
    k, v = jnp.split(kv, 2, axis=-1)
    k = k.reshape(b, s, DIFF_HEADS, 2, DIFF_HEAD_DIM).transpose(0, 2, 3, 1, 4)
    v = v.reshape(b, s, DIFF_HEADS, 2 * DIFF_HEAD_DIM).transpose(0, 2, 1, 3)
    return k, v


def diff_attention(xn, k_sh, v_sh, w_q, lam_params, subln_g, w_out, lambda_init):
    b, s, d = xn.shape
    H, dh, QB = DIFF_HEADS, DIFF_HEAD_DIM, DIFF_Q_BLOCK
    nq = s // QB
    f32 = jnp.float32
    q = (xn @ w_q).astype(f32) * (dh ** -0.5)
    q = q.reshape(b, nq, QB, H, 2, dh).transpose(1, 0, 3, 4, 2, 5)
    lp = lam_params.astype(f32)
    lam = jnp.exp(jnp.sum(lp[0] * lp[1])) - jnp.exp(jnp.sum(lp[2] * lp[3])) + lambda_init
    key_pos = jnp.arange(s)

    def block(args):
        qb, i = args
        scores = jnp.einsum('bhiqd,bhikd->bhiqk', qb, k_sh)
        q_pos = i * QB + jnp.arange(QB)
        causal = key_pos[None, :] <= q_pos[:, None]
        p = jax.nn.softmax(jnp.where(causal, scores, -jnp.inf), axis=-1)
        a = p[:, :, 0] - lam * p[:, :, 1]
        return jnp.einsum('bhqk,bhkv->bhqv', a, v_sh)

    o = lax.map(block, (q, jnp.arange(nq)))
    o = o.transpose(1, 0, 3, 2, 4).reshape(b, s, H, 2 * dh)
    o = rms_norm(o, subln_g, HEAD_NORM_EPS) * (1.0 - lambda_init)
    return o.reshape(b, s, d).astype(xn.dtype) @ w_out


def swiglu(xn, w_gate_up, w_down):
    gate, up = jnp.split(xn @ w_gate_up, 2, axis=-1)
    return (jax.nn.silu(gate) * up) @ w_down


def setup_inputs(seed: int = 0) -> dict:
    key = jax.random.key(seed)
    ks = jax.random.split(key, 20)
    D, F = D_MODEL, FFN_HIDDEN
    nA, nB = N_A_LAYERS, N_B_LAYERS

    def w(k, shape, fan_in):
        return jax.random.normal(k, shape, jnp.float32) * (fan_in ** -0.5)

    def gain(k, shape):
        return 1.0 + 0.02 * jax.random.normal(k, shape, jnp.float32)

    return {
        'x': jax.random.normal(ks[0], (BATCH, SEQ, D), jnp.float32),
        'ret_norm_pre': gain(ks[1], (nA, D)),
        'ret_norm_post': gain(ks[2], (nA, D)),
        'ret_w_in': w(ks[3], (nA, D, 4 * D), D),
        'ret_w_out': w(ks[4], (nA, D, D), D),
        'kv_norm': gain(ks[5], (D,)),
        'kv_w': w(ks[6], (D, 2 * D), D),
        'dif_norm_pre': gain(ks[7], (nB, D)),
        'dif_norm_post': gain(ks[8], (nB, D)),
        'dif_w_q': w(ks[9], (nB, D, D), D),
        'dif_lambda': 0.1 * jax.random.normal(ks[10], (nB, 4, DIFF_HEAD_DIM), jnp.float32),
        'dif_subln': gain(ks[11], (nB, 2 * DIFF_HEAD_DIM)),
        'dif_w_out': w(ks[12], (nB, D, D), D),
        'ffn_norm_pre': gain(ks[13], (DEPTH, D)),
        'ffn_norm_post': gain(ks[14], (DEPTH, D)),
        'ffn_w_gate_up': w(ks[15], (DEPTH, D, 2 * F), D),
        'ffn_w_down': w(ks[16], (DEPTH, F, D), F),
    }


def reference(x, ret_norm_pre, ret_norm_post, ret_w_in, ret_w_out, kv_norm, kv_w,
              dif_norm_pre, dif_norm_post, dif_w_q, dif_lambda, dif_subln, dif_w_out,
              ffn_norm_pre, ffn_norm_post, ffn_w_gate_up, ffn_w_down):
    h = x
    k_sh = v_sh = None
    for l in range(DEPTH):
        if l < N_A_LAYERS:
            a = l
            mix = retention(rms_norm(h, ret_norm_pre[a]), ret_w_in[a], ret_w_out[a])
            h = h + rms_norm(mix, ret_norm_post[a])
        else:
            bl = l - N_A_LAYERS
            if l == N_A_LAYERS:
                k_sh, v_sh = shared_kv(h, kv_norm, kv_w)
            lambda_init = 0.8 - 0.6 * math.exp(-0.3 * l)
            mix = diff_attention(rms_norm(h, dif_norm_pre[bl]), k_sh, v_sh, dif_w_q[bl],
                                 dif_lambda[bl], dif_subln[bl], dif_w_out[bl], lambda_init)
            h = h + rms_norm(mix, dif_norm_post[bl])
        ffn = swiglu(rms_norm(h, ffn_norm_pre[l]), ffn_w_gate_up[l], ffn_w_down[l])
        h = h + rms_norm(ffn, ffn_norm_post[l])
    return h
```

```python
import functools
import math

import jax
import jax.numpy as jnp
from jax import lax
from jax.experimental import pallas as pl
from jax.experimental.pallas import tpu as pltpu

RET_HEAD_DIM = 256
RET_DECAY_BASE = 5.0
ROPE_BASE = 10000.0
DIFF_HEAD_DIM = 128
NORM_EPS = 1e-6
HEAD_NORM_EPS = 1e-5
LANES = 128
NEG_BIG = -1e30

MM_BM = 1024
MM_BN = 1024
FFN_BN = 512
DOWN_BK = 2816
ROW_BLOCK = 256
RET_CHUNK = 256
RET_ROWS = 512
ATT_BQ = 512
VMEM_LIMIT = 56 * 1024 * 1024

F32 = jnp.float32
BF16 = jnp.bfloat16


def _pick(dim, pref, mult):
    b = min(pref, dim)
    b -= b % mult
    while b > mult and dim % b:
        b -= mult
    assert b > 0 and dim % b == 0, (dim, pref, mult)
    return b


def _params(sem):
    return pltpu.CompilerParams(dimension_semantics=sem, vmem_limit_bytes=VMEM_LIMIT)


def _rms(x, g, eps):
    return x * lax.rsqrt(jnp.mean(x * x, axis=-1, keepdims=True) + eps) * g


def _norm_cast_kernel(x_ref, g_ref, o_ref):
    o_ref[...] = _rms(x_ref[...], g_ref[...], NORM_EPS).astype(o_ref.dtype)


def norm_cast(x, g):
    n, d = x.shape
    br = _pick(n, ROW_BLOCK, 8)
    return pl.pallas_call(
        _norm_cast_kernel,
        grid=(n // br,),
        in_specs=[pl.BlockSpec((br, d), lambda i: (i, 0)), pl.BlockSpec((1, d), lambda i: (0, 0))],
        out_specs=pl.BlockSpec((br, d), lambda i: (i, 0)),
        out_shape=jax.ShapeDtypeStruct((n, d), BF16),
        compiler_params=_params(("parallel",)),
        name="norm_cast",
    )(x, g.reshape(1, d))


def _resid_norm_kernel(h_ref, mix_ref, gpost_ref, gnext_ref, hnew_ref, *xn_refs):
    h = h_ref[...] + _rms(mix_ref[...], gpost_ref[...], NORM_EPS)
    hnew_ref[...] = h
    for t, xn_ref in enumerate(xn_refs):
        xn_ref[...] = _rms(h, gnext_ref[t:t + 1, :], NORM_EPS).astype(xn_ref.dtype)


def resid_norm(h, mix, g_post, g_next):
    n, d = h.shape
    n_next = len(g_next)
    br = _pick(n, ROW_BLOCK, 8)
    row = pl.BlockSpec((br, d), lambda i: (i, 0))
    g_next_arr = jnp.stack(g_next) if n_next else jnp.ones((1, d), F32)
    outs = pl.pallas_call(
        _resid_norm_kernel,
        grid=(n // br,),
        in_specs=[row, row, pl.BlockSpec((1, d), lambda i: (0, 0)),
                  pl.BlockSpec(g_next_arr.shape, lambda i: (0, 0))],
        out_specs=[row] * (1 + n_next),
        out_shape=[jax.ShapeDtypeStruct((n, d), F32)] + [jax.ShapeDtypeStruct((n, d), BF16)] * n_next,
        compiler_params=_params(("parallel",)),
        name="resid_norm",
    )(h, mix, g_post.reshape(1, d), g_next_arr)
    return outs


def _mm_kernel(x_ref, w_ref, o_ref, *, scale):
    acc = jnp.dot(x_ref[...], w_ref[...], preferred_element_type=F32)
    if scale is not None:
        acc = acc * scale
    o_ref[...] = acc.astype(o_ref.dtype)


def matmul(x, w, out_dtype, scale=None):
    m, k = x.shape
    _, n = w.shape
    bm, bn = _pick(m, MM_BM, 8), _pick(n, MM_BN, LANES)
    return pl.pallas_call(
        functools.partial(_mm_kernel, scale=scale),
        grid=(m // bm, n // bn),
        in_specs=[pl.BlockSpec((bm, k), lambda i, j: (i, 0)), pl.BlockSpec((k, bn), lambda i, j: (0, j))],
        out_specs=pl.BlockSpec((bm, bn), lambda i, j: (i, j)),
        out_shape=jax.ShapeDtypeStruct((m, n), out_dtype),
        compiler_params=_params(("parallel", "parallel")),
        name="matmul",
    )(x, w)


def _mm_ksplit_kernel(x_ref, w_ref, o_ref, acc_ref):
    kk = pl.program_id(2)
    part = jnp.dot(x_ref[...], w_ref[...], preferred_element_type=F32)

    @pl.when(kk == 0)
    def _():
        acc_ref[...] = part

    @pl.when(kk > 0)
    def _():
        acc_ref[...] += part

    @pl.when(kk == pl.num_programs(2) - 1)
    def _():
        o_ref[...] = acc_ref[...].astype(o_ref.dtype)


def matmul_ksplit(x, w, out_dtype):
    m, k = x.shape
    _, n = w.shape
    bm, bn, bk = _pick(m, MM_BM, 8), _pick(n, MM_BN, LANES), _pick(k, DOWN_BK, LANES)
    return pl.pallas_call(
        _mm_ksplit_kernel,
        grid=(m // bm, n // bn, k // bk),
        in_specs=[pl.BlockSpec((bm, bk), lambda i, j, kk: (i, kk)), pl.BlockSpec((bk, bn), lambda i, j, kk: (kk, j))],
        out_specs=pl.BlockSpec((bm, bn), lambda i, j, kk: (i, j)),
        out_shape=jax.ShapeDtypeStruct((m, n), out_dtype),
        scratch_shapes=[pltpu.VMEM((bm, bn), F32)],
        compiler_params=_params(("parallel", "parallel", "arbitrary")),
        name="matmul_ksplit",
    )(x, w)


def _mm_swiglu_kernel(x_ref, wg_ref, wu_ref, o_ref):
    x = x_ref[...]
    gate = jnp.dot(x, wg_ref[...], preferred_element_type=F32)
    up = jnp.dot(x, wu_ref[...], preferred_element_type=F32)
    o_ref[...] = (gate * jax.nn.sigmoid(gate) * up).astype(o_ref.dtype)


def matmul_swiglu(x, wg, wu):
    m, k = x.shape
    _, n = wg.shape
    bm, bn = _pick(m, MM_BM, 8), _pick(n, FFN_BN, LANES)
    wspec = pl.BlockSpec((k, bn), lambda i, j: (0, j))
    return pl.pallas_call(
        _mm_swiglu_kernel,
        grid=(m // bm, n // bn),
        in_specs=[pl.BlockSpec((bm, k), lambda i, j: (i, 0)), wspec, wspec],
        out_specs=pl.BlockSpec((bm, bn), lambda i, j: (i, j)),
        out_shape=jax.ShapeDtypeStruct((m, n), BF16),
        compiler_params=_params(("parallel", "parallel")),
        name="matmul_swiglu",
    )(x, wg, wu)


def _mm_rotary_kernel(x_ref, w_ref, cos_ref, sin_ref, o_ref, *, n_q_blocks, n_rot_blocks, k_scale):
    j = pl.program_id(1)
    acc = jnp.dot(x_ref[...], w_ref[...], preferred_element_type=F32)
    bm, bn = acc.shape

    @pl.when(j < n_rot_blocks)
    def _():
        scale = jnp.where(j >= n_q_blocks, k_scale, 1.0).astype(F32)
        lane = lax.broadcasted_iota(jnp.int32, (bm, LANES), 1)
        even = (lane % 2) == 0
        for c in range(bn // LANES):
            a = acc[:, c * LANES:(c + 1) * LANES]
            off = (c * LANES) % RET_HEAD_DIM
            cs = cos_ref[:, off:off + LANES]
            sn = sin_ref[:, off:off + LANES]
            partner = jnp.where(even, pltpu.roll(a, LANES - 1, 1), pltpu.roll(a, 1, 1))
            o_ref[:, c * LANES:(c + 1) * LANES] = ((a * cs + partner * sn) * scale).astype(o_ref.dtype)

    @pl.when(j >= n_rot_blocks)
    def _():
        o_ref[...] = acc.astype(o_ref.dtype)


def matmul_rotary(x, w, cos_rep, sin_signed, seq):
    m, k = x.shape
    _, n = w.shape
    d = n // 4
    bm = _pick(seq, MM_BM, 8)
    bn = _pick(d, MM_BN, RET_HEAD_DIM)
    seq_blocks = seq // bm
    return pl.pallas_call(
        functools.partial(_mm_rotary_kernel, n_q_blocks=d // bn, n_rot_blocks=2 * d // bn,
                          k_scale=RET_HEAD_DIM ** -0.5),
        grid=(m // bm, n // bn),
        in_specs=[pl.BlockSpec((bm, k), lambda i, j: (i, 0)),
                  pl.BlockSpec((k, bn), lambda i, j: (0, j)),
                  pl.BlockSpec((bm, RET_HEAD_DIM), lambda i, j: (i % seq_blocks, 0)),
                  pl.BlockSpec((bm, RET_HEAD_DIM), lambda i, j: (i % seq_blocks, 0))],
        out_specs=pl.BlockSpec((bm, bn), lambda i, j: (i, j)),
        out_shape=jax.ShapeDtypeStruct((m, n), BF16),
        compiler_params=_params(("parallel", "parallel")),
        name="matmul_rotary",
    )(x, w, cos_rep, sin_signed)


def _retention_kernel(lg_ref, q_ref, k_ref, v_ref, g_ref, o_ref, state_ref, intra_ref, qd_ref, kd_ref, *, chunk):
    h = pl.program_id(1)
    lg = lg_ref[h]
    dh = RET_HEAD_DIM

    @pl.when(pl.program_id(2) == 0)
    def _():
        state_ref[...] = jnp.zeros_like(state_ref)
        rel = (lax.broadcasted_iota(jnp.int32, (chunk, chunk), 0)
               - lax.broadcasted_iota(jnp.int32, (chunk, chunk), 1)).astype(F32)
        intra_ref[...] = jnp.where(rel >= 0, jnp.exp(lg * rel), 0.0)
        pos = lax.broadcasted_iota(jnp.int32, (chunk, dh), 0).astype(F32)
        qd_ref[...] = jnp.exp(lg * (pos + 1.0))
        kd_ref[...] = jnp.exp(lg * (chunk - 1.0 - pos))

    chunk_decay = jnp.exp(jnp.full((1, dh), lg * chunk, F32))
    for c in range(q_ref.shape[0] // chunk):
        rows = pl.ds(c * chunk, chunk)
        q, k, v = q_ref[rows, :], k_ref[rows, :], v_ref[rows, :]
        scores = lax.dot_general(q, k, (((1,), (1,)), ((), ())), preferred_element_type=F32)
        scores = (scores * intra_ref[...]).astype(BF16)
        state = state_ref[...]
        q_dec = (q.astype(F32) * qd_ref[...]).astype(BF16)
        out = (jnp.dot(scores, v, preferred_element_type=F32)
               + jnp.dot(q_dec, state.astype(BF16), preferred_element_type=F32))
        k_dec = (k.astype(F32) * kd_ref[...]).astype(BF16)
        state_ref[...] = state * chunk_decay + lax.dot_general(
            k_dec, v, (((0,), (0,)), ((), ())), preferred_element_type=F32)
        mu = jnp.mean(out, axis=-1, keepdims=True)
        cen = out - mu
        var = jnp.mean(cen * cen, axis=-1, keepdims=True)
        gate = g_ref[rows, :].astype(F32)
        o_ref[rows, :] = (cen * lax.rsqrt(var + HEAD_NORM_EPS) * (gate * jax.nn.sigmoid(gate))).astype(o_ref.dtype)


def retention(qkvg, batch, seq):
    n, d4 = qkvg.shape
    d = d4 // 4
    heads = d // RET_HEAD_DIM
    rows = _pick(seq, RET_ROWS, 8)
    chunk = _pick(rows, RET_CHUNK, 8)
    steps = seq // rows
    log_gamma = jnp.log1p(-jnp.exp2(-RET_DECAY_BASE - jnp.arange(heads, dtype=F32)))

    def spec(part):
        return pl.BlockSpec((rows, RET_HEAD_DIM), lambda b, h, c: (b * steps + c, part * heads + h))

    return pl.pallas_call(
        functools.partial(_retention_kernel, chunk=chunk),
        grid=(batch, heads, steps),
        in_specs=[pl.BlockSpec(memory_space=pltpu.SMEM), spec(0), spec(1), spec(2), spec(3)],
        out_specs=spec(0),
        out_shape=jax.ShapeDtypeStruct((n, d), BF16),
        scratch_shapes=[pltpu.VMEM((RET_HEAD_DIM, RET_HEAD_DIM), F32),
                        pltpu.VMEM((chunk, chunk), F32),
                        pltpu.VMEM((chunk, RET_HEAD_DIM), F32),
                        pltpu.VMEM((chunk, RET_HEAD_DIM), F32)],
        compiler_params=_params(("parallel", "parallel", "arbitrary")),
        name="retention",
    )(log_gamma, qkvg, qkvg, qkvg, qkvg)


def _diff_attn_kernel(q_ref, k_ref, v_ref, lam_ref, g_ref, o_ref, m_ref, l_ref, acc_ref, *, lambda_init):
    qi = pl.program_id(2)
    bq = q_ref.shape[0]
    dh = DIFF_HEAD_DIM

    m_ref[...] = jnp.full_like(m_ref, NEG_BIG)
    l_ref[...] = jnp.zeros_like(l_ref)
    acc_ref[...] = jnp.zeros_like(acc_ref)

    def block(kb, masked):
        rows = pl.ds(pl.multiple_of(kb * bq, bq), bq)
        v = v_ref[rows, :]
        for i in range(2):
            q = q_ref[:, i * dh:(i + 1) * dh]
            k = k_ref[rows, i * dh:(i + 1) * dh]
            s = lax.dot_general(q, k, (((1,), (1,)), ((), ())), preferred_element_type=F32)
            if masked:
                keep = (lax.broadcasted_iota(jnp.int32, (bq, bq), 1)
                        <= lax.broadcasted_iota(jnp.int32, (bq, bq), 0))
                s = jnp.where(keep, s, NEG_BIG)
            m_old = m_ref[i]
            m_new = jnp.maximum(m_old, jnp.max(s, axis=-1, keepdims=True))
            alpha = jnp.exp(m_old - m_new)
            p = jnp.exp(s - m_new)
            l_ref[i] = alpha * l_ref[i] + jnp.sum(p, axis=-1, keepdims=True)
            acc_ref[i] = alpha * acc_ref[i] + jnp.dot(p.astype(BF16), v, preferred_element_type=F32)
            m_ref[i] = m_new

    def full_block(kb, carry):
        block(kb, masked=False)
        return carry

    lax.fori_loop(0, qi, full_block, 0)
    block(qi, masked=True)

    lp = lam_ref[...]
    lam = (jnp.exp(jnp.sum(lp[0:1] * lp[1:2], axis=-1, keepdims=True))
           - jnp.exp(jnp.sum(lp[2:3] * lp[3:4], axis=-1, keepdims=True)) + lambda_init)
    o = acc_ref[0] / l_ref[0] - lam * (acc_ref[1] / l_ref[1])
    o_ref[...] = (_rms(o, g_ref[...], HEAD_NORM_EPS) * (1.0 - lambda_init)).astype(o_ref.dtype)


def diff_attention(q, kv, lam_params, subln_g, batch, seq, lambda_init):
    n, d = q.shape
    hw = 2 * DIFF_HEAD_DIM
    heads = d // hw
    bq = _pick(seq, ATT_BQ, 8)
    nq = seq // bq
    return pl.pallas_call(
        functools.partial(_diff_attn_kernel, lambda_init=lambda_init),
        grid=(batch, heads, nq),
        in_specs=[pl.BlockSpec((bq, hw), lambda b, h, i: (b * nq + i, h)),
                  pl.BlockSpec((seq, hw), lambda b, h, i: (b, h)),
                  pl.BlockSpec((seq, hw), lambda b, h, i: (b, heads + h)),
                  pl.BlockSpec(lam_params.shape, lambda b, h, i: (0, 0)),
                  pl.BlockSpec((1, hw), lambda b, h, i: (0, 0))],
        out_specs=pl.BlockSpec((bq, hw), lambda b, h, i: (b * nq + i, h)),
        out_shape=jax.ShapeDtypeStruct((n, d), BF16),
        scratch_shapes=[pltpu.VMEM((2, bq, 1), F32), pltpu.VMEM((2, bq, 1), F32), pltpu.VMEM((2, bq, hw), F32)],
        compiler_params=_params(("parallel", "parallel", "arbitrary")),
        name="diff_attention",
    )(q, kv, kv, lam_params, subln_g.reshape(1, hw))


def _rotary_tables(seq):
    half = RET_HEAD_DIM // 2
    inv_freq = 1.0 / (ROPE_BASE ** jnp.linspace(0.0, 1.0, half, dtype=F32))
    ang = jnp.arange(seq, dtype=F32)[:, None] * inv_freq[None, :]
    cos, sin = jnp.cos(ang), jnp.sin(ang)
    cos_rep = jnp.stack([cos, cos], axis=-1).reshape(seq, RET_HEAD_DIM)
    sin_signed = jnp.stack([-sin, sin], axis=-1).reshape(seq, RET_HEAD_DIM)
    return cos_rep, sin_signed


def _ffn(xn, w_gate_up, w_down):
    f = w_down.shape[0]
    fp = -(-f // FFN_BN) * FFN_BN
    pad = fp - f
    wg = jnp.pad(w_gate_up[:, :f].astype(BF16), ((0, 0), (0, pad)))
    wu = jnp.pad(w_gate_up[:, f:].astype(BF16), ((0, 0), (0, pad)))
    wd = jnp.pad(w_down.astype(BF16), ((0, pad), (0, 0)))
    hidden = matmul_swiglu(xn, wg, wu)
    return matmul_ksplit(hidden, wd, F32)


def kernel(x, ret_norm_pre, ret_norm_post, ret_w_in, ret_w_out, kv_norm, kv_w, dif_norm_pre, dif_norm_post, dif_w_q, dif_lambda, dif_subln, dif_w_out, ffn_norm_pre, ffn_norm_post, ffn_w_gate_up, ffn_w_down):
    batch, seq, d = x.shape
    n = batch * seq
    h = x.reshape(n, d)
    cos_rep, sin_signed = _rotary_tables(seq)

    xn = norm_cast(h, ret_norm_pre[0])
    qkvg = matmul_rotary(xn, ret_w_in[0].astype(BF16), cos_rep, sin_signed, seq)
    ret = retention(qkvg, batch, seq)
    mix = matmul(ret, ret_w_out[0].astype(BF16), F32)
    h, xn = resid_norm(h, mix, ret_norm_post[0], [ffn_norm_pre[0]])
    ffn = _ffn(xn, ffn_w_gate_up[0], ffn_w_down[0])
    h, xkv, xq = resid_norm(h, ffn, ffn_norm_post[0], [kv_norm, dif_norm_pre[0]])

    lambda_init = 0.8 - 0.6 * math.exp(-0.3 * 1)
    kv = matmul(xkv, kv_w.astype(BF16), BF16)
    q = matmul(xq, dif_w_q[0].astype(BF16), BF16, scale=DIFF_HEAD_DIM ** -0.5)
    att = diff_attention(q, kv, dif_lambda[0], dif_subln[0], batch, seq, lambda_init)
    mix = matmul(att, dif_w_out[0].astype(BF16), F32)
    h, xn = resid_norm(h, mix, dif_norm_post[0], [ffn_norm_pre[1]])
    ffn = _ffn(xn, ffn_w_gate_up[1], ffn_w_down[1])
    (h,) = resid_norm(h, ffn, ffn_norm_post[1], [])
    return h.reshape(batch, seq, d)
```

```python
import functools
import math

import jax
import jax.numpy as jnp
from jax import lax
from jax.experimental import pallas as pl
from jax.experimental.pallas import tpu as pltpu

RET_HEAD_DIM = 256
RET_DECAY_BASE = 5.0
ROPE_BASE = 10000.0
DIFF_HEAD_DIM = 128
NORM_EPS = 1e-6
HEAD_NORM_EPS = 1e-5
LANES = 128
NEG_BIG = -1e30

MM_BM = 1024
MM_BN = 1024
FFN_BN = 512
DOWN_BK = 2816
ROW_BLOCK = 256
RET_CHUNK = 256
RET_ROWS = 512
ATT_BQ = 512
VMEM_LIMIT = 56 * 1024 * 1024

F32 = jnp.float32
BF16 = jnp.bfloat16


def _pick(dim, pref, mult):
    b = min(pref, dim)
    b -= b % mult
    while b > mult and dim % b:
        b -= mult
    assert b > 0 and dim % b == 0, (dim, pref, mult)
    return b


def _params(sem):
    return pltpu.CompilerParams(dimension_semantics=sem, vmem_limit_bytes=VMEM_LIMIT)


def _rms(x, g, eps):
    return x * lax.rsqrt(jnp.mean(x * x, axis=-1, keepdims=True) + eps) * g


def _norm_cast_kernel(x_ref, g_ref, o_ref):
    o_ref[...] = _rms(x_ref[...], g_ref[...], NORM_EPS).astype(o_ref.dtype)


def norm_cast(x, g):
    n, d = x.shape
    br = _pick(n, ROW_BLOCK, 8)
    return pl.pallas_call(
        _norm_cast_kernel,
        grid=(n // br,),
        in_specs=[pl.BlockSpec((br, d), lambda i: (i, 0)), pl.BlockSpec((1, d), lambda i: (0, 0))],
        out_specs=pl.BlockSpec((br, d), lambda i: (i, 0)),
        out_shape=jax.ShapeDtypeStruct((n, d), BF16),
        compiler_params=_params(("parallel",)),
        name="norm_cast",
    )(x, g.reshape(1, d))


def _resid_norm_kernel(h_ref, mix_ref, gpost_ref, gnext_ref, hnew_ref, *xn_refs):
    h = h_ref[...] + _rms(mix_ref[...], gpost_ref[...], NORM_EPS)
    hnew_ref[...] = h
    for t, xn_ref in enumerate(xn_refs):
        xn_ref[...] = _rms(h, gnext_ref[t:t + 1, :], NORM_EPS).astype(xn_ref.dtype)


def resid_norm(h, mix, g_post, g_next):
    n, d = h.shape
    n_next = len(g_next)
    br = _pick(n, ROW_BLOCK, 8)
    row = pl.BlockSpec((br, d), lambda i: (i, 0))
    g_next_arr = jnp.stack(g_next) if n_next else jnp.ones((1, d), F32)
    outs = pl.pallas_call(
        _resid_norm_kernel,
        grid=(n // br,),
        in_specs=[row, row, pl.BlockSpec((1, d), lambda i: (0, 0)),
                  pl.BlockSpec(g_next_arr.shape, lambda i: (0, 0))],
        out_specs=[row] * (1 + n_next),
        out_shape=[jax.ShapeDtypeStruct((n, d), F32)] + [jax.ShapeDtypeStruct((n, d), BF16)] * n_next,
        compiler_params=_params(("parallel",)),
        name="resid_norm",
    )(h, mix, g_post.reshape(1, d), g_next_arr)
    return outs


def _mm_kernel(x_ref, w_ref, o_ref, *, scale):
    acc = jnp.dot(x_ref[...], w_ref[...], preferred_element_type=F32)
    if scale is not None:
        acc = acc * scale
    o_ref[...] = acc.astype(o_ref.dtype)


def matmul(x, w, out_dtype, scale=None):
    m, k = x.shape
    _, n = w.shape
    bm, bn = _pick(m, MM_BM, 8), _pick(n, MM_BN, LANES)
    return pl.pallas_call(
        functools.partial(_mm_kernel, scale=scale),
        grid=(m // bm, n // bn),
        in_specs=[pl.BlockSpec((bm, k), lambda i, j: (i, 0)), pl.BlockSpec((k, bn), lambda i, j: (0, j))],
        out_specs=pl.BlockSpec((bm, bn), lambda i, j: (i, j)),
        out_shape=jax.ShapeDtypeStruct((m, n), out_dtype),
        compiler_params=_params(("parallel", "parallel")),
        name="matmul",
    )(x, w)


def _mm_ksplit_kernel(x_ref, w_ref, o_ref, acc_ref):
    kk = pl.program_id(2)
    part = jnp.dot(x_ref[...], w_ref[...], preferred_element_type=F32)

    @pl.when(kk == 0)
    def _():
        acc_ref[...] = part

    @pl.when(kk > 0)
    def _():
        acc_ref[...] += part

    @pl.when(kk == pl.num_programs(2) - 1)
    def _():
        o_ref[...] = acc_ref[...].astype(o_ref.dtype)


def matmul_ksplit(x, w, out_dtype):
    m, k = x.shape
    _, n = w.shape
    bm, bn, bk = _pick(m, MM_BM, 8), _pick(n, MM_BN, LANES), _pick(k, DOWN_BK, LANES)
    return pl.pallas_call(
        _mm_ksplit_kernel,
        grid=(m // bm, n // bn, k // bk),
        in_specs=[pl.BlockSpec((bm, bk), lambda i, j, kk: (i, kk)), pl.BlockSpec((bk, bn), lambda i, j, kk: (kk, j))],
        out_specs=pl.BlockSpec((bm, bn), lambda i, j, kk: (i, j)),
        out_shape=jax.ShapeDtypeStruct((m, n), out_dtype),
        scratch_shapes=[pltpu.VMEM((bm, bn), F32)],
        compiler_params=_params(("parallel", "parallel", "arbitrary")),
        name="matmul_ksplit",
    )(x, w)


def _mm_swiglu_kernel(x_ref, wg_ref, wu_ref, o_ref):
    x = x_ref[...]
    gate = jnp.dot(x, wg_ref[...], preferred_element_type=F32)
    up = jnp.dot(x, wu_ref[...], preferred_element_type=F32)
    o_ref[...] = (gate * jax.nn.sigmoid(gate) * up).astype(o_ref.dtype)


def matmul_swiglu(x, w_gate_up):
    m, k = x.shape
    n = w_gate_up.shape[1] // 2
    bm, bn = _pick(m, MM_BM, 8), _pick(n, FFN_BN, LANES)
    nj = n // bn
    return pl.pallas_call(
        _mm_swiglu_kernel,
        grid=(m // bm, nj),
        in_specs=[pl.BlockSpec((bm, k), lambda i, j: (i, 0)),
                  pl.BlockSpec((k, bn), lambda i, j: (0, j)),
                  pl.BlockSpec((k, bn), lambda i, j: (0, nj + j))],
        out_specs=pl.BlockSpec((bm, bn), lambda i, j: (i, j)),
        out_shape=jax.ShapeDtypeStruct((m, n), BF16),
        compiler_params=_params(("parallel", "parallel")),
        name="matmul_swiglu",
    )(x, w_gate_up, w_gate_up)


def _mm_rotary_kernel(x_ref, w_ref, cos_ref, sin_ref, o_ref, *, n_q_blocks, n_rot_blocks, k_scale):
    j = pl.program_id(1)
    acc = jnp.dot(x_ref[...], w_ref[...], preferred_element_type=F32)
    bm, bn = acc.shape

    @pl.when(j < n_rot_blocks)
    def _():
        scale = jnp.where(j >= n_q_blocks, k_scale, 1.0).astype(F32)
        lane = lax.broadcasted_iota(jnp.int32, (bm, LANES), 1)
        even = (lane % 2) == 0
        for c in range(bn // LANES):
            a = acc[:, c * LANES:(c + 1) * LANES]
            off = (c * LANES) % RET_HEAD_DIM
            cs = cos_ref[:, off:off + LANES]
            sn = sin_ref[:, off:off + LANES]
            partner = jnp.where(even, pltpu.roll(a, LANES - 1, 1), pltpu.roll(a, 1, 1))
            o_ref[:, c * LANES:(c + 1) * LANES] = ((a * cs + partner * sn) * scale).astype(o_ref.dtype)

    @pl.when(j >= n_rot_blocks)
    def _():
        o_ref[...] = acc.astype(o_ref.dtype)


def matmul_rotary(x, w, cos_rep, sin_signed, seq):
    m, k = x.shape
    _, n = w.shape
    d = n // 4
    bm = _pick(seq, MM_BM, 8)
    bn = _pick(d, MM_BN, RET_HEAD_DIM)
    seq_blocks = seq // bm
    return pl.pallas_call(
        functools.partial(_mm_rotary_kernel, n_q_blocks=d // bn, n_rot_blocks=2 * d // bn,
                          k_scale=RET_HEAD_DIM ** -0.5),
        grid=(m // bm, n // bn),
        in_specs=[pl.BlockSpec((bm, k), lambda i, j: (i, 0)),
                  pl.BlockSpec((k, bn), lambda i, j: (0, j)),
                  pl.BlockSpec((bm, RET_HEAD_DIM), lambda i, j: (i % seq_blocks, 0)),
                  pl.BlockSpec((bm, RET_HEAD_DIM), lambda i, j: (i % seq_blocks, 0))],
        out_specs=pl.BlockSpec((bm, bn), lambda i, j: (i, j)),
        out_shape=jax.ShapeDtypeStruct((m, n), BF16),
        compiler_params=_params(("parallel", "parallel")),
        name="matmul_rotary",
    )(x, w, cos_rep, sin_signed)


def _retention_kernel(lg_ref, q_ref, k_ref, v_ref, g_ref, o_ref, state_ref, intra_ref, qd_ref, kd_ref, *, chunk):
    h = pl.program_id(1)
    lg = lg_ref[h]
    dh = RET_HEAD_DIM

    @pl.when(pl.program_id(2) == 0)
    def _():
        state_ref[...] = jnp.zeros_like(state_ref)
        rel = (lax.broadcasted_iota(jnp.int32, (chunk, chunk), 0)
               - lax.broadcasted_iota(jnp.int32, (chunk, chunk), 1)).astype(F32)
        intra_ref[...] = jnp.where(rel >= 0, jnp.exp(lg * rel), 0.0)
        pos = lax.broadcasted_iota(jnp.int32, (chunk, dh), 0).astype(F32)
        qd_ref[...] = jnp.exp(lg * (pos + 1.0))
        kd_ref[...] = jnp.exp(lg * (chunk - 1.0 - pos))

    chunk_decay = jnp.exp(jnp.full((1, dh), lg * chunk, F32))
    for c in range(q_ref.shape[0] // chunk):
        rows = pl.ds(c * chunk, chunk)
        q, k, v = q_ref[rows, :], k_ref[rows, :], v_ref[rows, :]
        scores = lax.dot_general(q, k, (((1,), (1,)), ((), ())), preferred_element_type=F32)
        scores = (scores * intra_ref[...]).astype(BF16)
        state = state_ref[...]
        q_dec = (q.astype(F32) * qd_ref[...]).astype(BF16)
        out = (jnp.dot(scores, v, preferred_element_type=F32)
               + jnp.dot(q_dec, state.astype(BF16), preferred_element_type=F32))
        k_dec = (k.astype(F32) * kd_ref[...]).astype(BF16)
        state_ref[...] = state * chunk_decay + lax.dot_general(
            k_dec, v, (((0,), (0,)), ((), ())), preferred_element_type=F32)
        mu = jnp.mean(out, axis=-1, keepdims=True)
        cen = out - mu
        var = jnp.mean(cen * cen, axis=-1, keepdims=True)
        gate = g_ref[rows, :].astype(F32)
        o_ref[rows, :] = (cen * lax.rsqrt(var + HEAD_NORM_EPS) * (gate * jax.nn.sigmoid(gate))).astype(o_ref.dtype)


def retention(qkvg, batch, seq):
    n, d4 = qkvg.shape
    d = d4 // 4
    heads = d // RET_HEAD_DIM
    rows = _pick(seq, RET_ROWS, 8)
    chunk = _pick(rows, RET_CHUNK, 8)
    steps = seq // rows
    log_gamma = jnp.log1p(-jnp.exp2(-RET_DECAY_BASE - jnp.arange(heads, dtype=F32)))

    def spec(part):
        return pl.BlockSpec((rows, RET_HEAD_DIM), lambda b, h, c: (b * steps + c, part * heads + h))

    return pl.pallas_call(
        functools.partial(_retention_kernel, chunk=chunk),
        grid=(batch, heads, steps),
        in_specs=[pl.BlockSpec(memory_space=pltpu.SMEM), spec(0), spec(1), spec(2), spec(3)],
        out_specs=spec(0),
        out_shape=jax.ShapeDtypeStruct((n, d), BF16),
        scratch_shapes=[pltpu.VMEM((RET_HEAD_DIM, RET_HEAD_DIM), F32),
                        pltpu.VMEM((chunk, chunk), F32),
                        pltpu.VMEM((chunk, RET_HEAD_DIM), F32),
                        pltpu.VMEM((chunk, RET_HEAD_DIM), F32)],
        compiler_params=_params(("parallel", "parallel", "arbitrary")),
        name="retention",
    )(log_gamma, qkvg, qkvg, qkvg, qkvg)


def _diff_attn_kernel(q_ref, k_ref, v_ref, lam_ref, g_ref, o_ref, m_ref, l_ref, acc_ref, *, lambda_init):
    qi = pl.program_id(2)
    bq = q_ref.shape[0]
    dh = DIFF_HEAD_DIM
    n_slabs = bq // LANES

    m_ref[...] = jnp.full_like(m_ref, NEG_BIG)
    l_ref[...] = jnp.zeros_like(l_ref)
    acc_ref[...] = jnp.zeros_like(acc_ref)

    def block(kb, masked):
        rows = pl.ds(pl.multiple_of(kb * bq, bq), bq)
        v = v_ref[rows, :]
        for i in range(2):
            q = q_ref[:, i * dh:(i + 1) * dh]
            k = k_ref[rows, i * dh:(i + 1) * dh]
            s = lax.dot_general(q, k, (((1,), (1,)), ((), ())), preferred_element_type=F32)
            if masked:
                keep = (lax.broadcasted_iota(jnp.int32, (bq, bq), 1)
                        <= lax.broadcasted_iota(jnp.int32, (bq, bq), 0))
                s = jnp.where(keep, s, NEG_BIG)
            m_old = m_ref[i]
            m_new = jnp.maximum(m_old, jnp.max(s, axis=-1, keepdims=True))
            alpha = jnp.exp2(m_old - m_new)
            p_slabs = [jnp.exp2(s[:, c * LANES:(c + 1) * LANES] - m_new) for c in range(n_slabs)]
            l_ref[i] = alpha * l_ref[i] + functools.reduce(lambda a, b: a + b, p_slabs)
            p = jnp.concatenate([t.astype(BF16) for t in p_slabs], axis=1)
            acc_ref[i] = (jnp.concatenate([alpha, alpha], axis=1) * acc_ref[i]
                          + jnp.dot(p, v, preferred_element_type=F32))
            m_ref[i] = m_new

    def full_block(kb, carry):
        block(kb, masked=False)
        return carry

    lax.fori_loop(0, qi, full_block, 0)
    block(qi, masked=True)

    lp = lam_ref[...]
    lam = (jnp.exp(jnp.sum(lp[0:1] * lp[1:2], axis=-1, keepdims=True))
           - jnp.exp(jnp.sum(lp[2:3] * lp[3:4], axis=-1, keepdims=True)) + lambda_init)
    inv_l0 = 1.0 / jnp.sum(l_ref[0], axis=-1, keepdims=True)
    inv_l1 = 1.0 / jnp.sum(l_ref[1], axis=-1, keepdims=True)
    o = acc_ref[0] * inv_l0 - (lam * inv_l1) * acc_ref[1]
    o_ref[...] = (_rms(o, g_ref[...], HEAD_NORM_EPS) * (1.0 - lambda_init)).astype(o_ref.dtype)


def diff_attention(q, kv, lam_params, subln_g, batch, seq, lambda_init):
    n, d = q.shape
    hw = 2 * DIFF_HEAD_DIM
    heads = d // hw
    bq = _pick(seq, ATT_BQ, 8)
    nq = seq // bq
    return pl.pallas_call(
        functools.partial(_diff_attn_kernel, lambda_init=lambda_init),
        grid=(batch, heads, nq),
        in_specs=[pl.BlockSpec((bq, hw), lambda b, h, i: (b * nq + i, h)),
                  pl.BlockSpec((seq, hw), lambda b, h, i: (b, h)),
                  pl.BlockSpec((seq, hw), lambda b, h, i: (b, heads + h)),
                  pl.BlockSpec(lam_params.shape, lambda b, h, i: (0, 0)),
                  pl.BlockSpec((1, hw), lambda b, h, i: (0, 0))],
        out_specs=pl.BlockSpec((bq, hw), lambda b, h, i: (b * nq + i, h)),
        out_shape=jax.ShapeDtypeStruct((n, d), BF16),
        scratch_shapes=[pltpu.VMEM((2, bq, LANES), F32), pltpu.VMEM((2, bq, LANES), F32),
                        pltpu.VMEM((2, bq, hw), F32)],
        compiler_params=_params(("parallel", "parallel", "arbitrary")),
        name="diff_attention",
    )(q, kv, kv, lam_params, subln_g.reshape(1, hw))


def _rotary_tables(seq):
    half = RET_HEAD_DIM // 2
    inv_freq = 1.0 / (ROPE_BASE ** jnp.linspace(0.0, 1.0, half, dtype=F32))
    ang = jnp.arange(seq, dtype=F32)[:, None] * inv_freq[None, :]
    cos, sin = jnp.cos(ang), jnp.sin(ang)
    cos_rep = jnp.stack([cos, cos], axis=-1).reshape(seq, RET_HEAD_DIM)
    sin_signed = jnp.stack([-sin, sin], axis=-1).reshape(seq, RET_HEAD_DIM)
    return cos_rep, sin_signed


def _cast_pad_kernel(x_ref, o_ref, *, axis, n_real):
    real = pl.program_id(axis) < n_real
    o_ref[...] = jnp.where(real, x_ref[...], 0.0).astype(o_ref.dtype)


def cast_pad_gate_up(w, f, fp):
    k = w.shape[0]
    cb = _pick(math.gcd(f, fp), 2 * LANES, LANES)
    n_real, n_out = f // cb, fp // cb
    return pl.pallas_call(
        functools.partial(_cast_pad_kernel, axis=1, n_real=n_real),
        grid=(2, n_out),
        in_specs=[pl.BlockSpec((k, cb), lambda hf, c: (0, hf * n_real + jnp.minimum(c, n_real - 1)))],
        out_specs=pl.BlockSpec((k, cb), lambda hf, c: (0, hf * n_out + c)),
        out_shape=jax.ShapeDtypeStruct((k, 2 * fp), BF16),
        compiler_params=_params(("parallel", "parallel")),
        name="cast_pad_gate_up",
    )(w)


def cast_pad_down(w, fp):
    f, d = w.shape
    rb = _pick(math.gcd(f, fp), 2 * LANES, 8)
    n_real = f // rb
    return pl.pallas_call(
        functools.partial(_cast_pad_kernel, axis=0, n_real=n_real),
        grid=(fp // rb,),
        in_specs=[pl.BlockSpec((rb, d), lambda r: (jnp.minimum(r, n_real - 1), 0))],
        out_specs=pl.BlockSpec((rb, d), lambda r: (r, 0)),
        out_shape=jax.ShapeDtypeStruct((fp, d), BF16),
        compiler_params=_params(("parallel",)),
        name="cast_pad_down",
    )(w)


def _ffn(xn, w_gate_up, w_down):
    f = w_down.shape[0]
    fp = -(-f // FFN_BN) * FFN_BN
    hidden = matmul_swiglu(xn, cast_pad_gate_up(w_gate_up, f, fp))
    return matmul_ksplit(hidden, cast_pad_down(w_down, fp), F32)


def kernel(x, ret_norm_pre, ret_norm_post, ret_w_in, ret_w_out, kv_norm, kv_w, dif_norm_pre, dif_norm_post, dif_w_q, dif_lambda, dif_subln, dif_w_out, ffn_norm_pre, ffn_norm_post, ffn_w_gate_up, ffn_w_down):
    batch, seq, d = x.shape
    n = batch * seq
    h = x.reshape(n, d)
    cos_rep, sin_signed = _rotary_tables(seq)

    xn = norm_cast(h, ret_norm_pre[0])
    qkvg = matmul_rotary(xn, ret_w_in[0].astype(BF16), cos_rep, sin_signed, seq)
    ret = retention(qkvg, batch, seq)
    mix = matmul(ret, ret_w_out[0].astype(BF16), F32)
    h, xn = resid_norm(h, mix, ret_norm_post[0], [ffn_norm_pre[0]])
    ffn = _ffn(xn, ffn_w_gate_up[0], ffn_w_down[0])
    h, xkv, xq = resid_norm(h, ffn, ffn_norm_post[0], [kv_norm, dif_norm_pre[0]])

    lambda_init = 0.8 - 0.6 * math.exp(-0.3 * 1)
    kv = matmul(xkv, kv_w.astype(BF16), BF16)
    q = matmul(xq, dif_w_q[0].astype(BF16), BF16, scale=DIFF_HEAD_DIM ** -0.5 * math.log2(math.e))
    att = diff_attention(q, kv, dif_lambda[0], dif_subln[0], batch, seq, lambda_init)
    mix = matmul(att, dif_w_out[0].astype(BF16), F32)
    h, xn = resid_norm(h, mix, dif_norm_post[0], [ffn_norm_pre[1]])
    ffn = _ffn(xn, ffn_w_gate_up[1], ffn_w_down[1])
    (h,) = resid_norm(h, ffn, ffn_norm_post[1], [])
    return h.reshape(batch, seq, d)
```
